```python
import jax, jax.numpy as jnp
from jax import lax
import numpy as np

D_MODEL = 2048
BATCH = 1
SEQ = 16384
DEPTH = 1

RMS_EPS = 1e-6
ATTN_HEADS = 16
ATTN_HEAD_DIM = 128
Q_LORA = 512
KV_LORA = 256
IDX_HEADS = 16
IDX_DIM = 64
INDEX_TOPK = 256
Q_BLOCK = 128
SSM_EXPAND = 2
SSM_INNER = SSM_EXPAND * D_MODEL
SSM_HEAD_DIM = 64
SSM_HEADS = SSM_INNER // SSM_HEAD_DIM
SSM_GROUPS = 8
SSM_HEADS_PER_GROUP = SSM_HEADS // SSM_GROUPS
SSM_STATE = 128
SSM_CONV = 4
SSM_CONV_DIM = SSM_INNER + 2 * SSM_GROUPS * SSM_STATE
SSM_CHUNK = 256
PEER_HEADS = 8
PEER_NKEYS = 128
PEER_EXPERTS = PEER_NKEYS * PEER_NKEYS
PEER_KEY_DIM = 128
PEER_TOPK = 16
PEER_BLOCK = 128
IN_WIDTHS = (Q_LORA, KV_LORA, IDX_DIM, IDX_HEADS, SSM_INNER, SSM_CONV_DIM, SSM_HEADS, D_MODEL, D_MODEL)
IN_WIDTH = sum(IN_WIDTHS)
IN_OFFSETS = tuple(int(o) for o in np.cumsum(IN_WIDTHS)[:-1])

kernel_name = "hybrid_dsa_mamba2_peer_block"


def rms_norm(x, g):
    xf = x.astype(jnp.float32)
    y = xf * lax.rsqrt(jnp.mean(xf * xf, axis=-1, keepdims=True) + RMS_EPS)
    return (y * g.astype(jnp.float32)).astype(x.dtype)


def to_blocks(a, blk):
    b, s = a.shape[:2]
    return a.reshape(b, s // blk, blk, *a.shape[2:]).swapaxes(0, 1)


def from_blocks(a):
    nb, b, blk = a.shape[:3]
    return a.swapaxes(0, 1).reshape(b, nb * blk, *a.shape[3:])


def sparse_attention(q, k, v, q_idx, k_idx, w_idx):
    b, s = q.shape[:2]
    k_top = min(INDEX_TOPK, s // 4)
    n_blk = s // Q_BLOCK
    s_pos = jnp.arange(s)
    scale = ATTN_HEAD_DIM ** -0.5
    idx_scale = IDX_DIM ** -0.5

    def block(args):
        t0, qb, qib, wib = args
        t_pos = t0 + jnp.arange(Q_BLOCK)
        logits = jnp.einsum('bqhd,bsd->bqhs', qib, k_idx)
        score = jnp.einsum('bqhs,bqh->bqs', jax.nn.relu(logits), wib).astype(jnp.float32) * idx_scale
        causal = s_pos[None, :] <= t_pos[:, None]
        score = jnp.where(causal[None], score, -jnp.inf)
        _, sel = lax.top_k(score, k_top)
        kg = jax.vmap(lambda kk, ii: kk[ii])(k, sel)
        vg = jax.vmap(lambda vv, ii: vv[ii])(v, sel)
        att = jnp.einsum('bqhd,bqkhd->bqhk', qb, kg).astype(jnp.float32) * scale
        valid = sel <= t_pos[None, :, None]
        att = jnp.where(valid[:, :, None, :], att, -jnp.inf)
        p = jax.nn.softmax(att, axis=-1).astype(vg.dtype)
        return jnp.einsum('bqhk,bqkhd->bqhd', p, vg)

    t0s = jnp.arange(n_blk) * Q_BLOCK
    out = lax.map(block, (t0s, to_blocks(q, Q_BLOCK), to_blocks(q_idx, Q_BLOCK), to_blocks(w_idx, Q_BLOCK)))
    return from_blocks(out)


def ssd_scan(xs, dt, a_neg, bm, cm):
    b, s = xs.shape[:2]
    pad = (-s) % SSM_CHUNK
    n_c = (s + pad) // SSM_CHUNK

    def chunks(a):
        a = jnp.pad(a, [(0, 0), (0, pad)] + [(0, 0)] * (a.ndim - 2))
        return a.reshape(b, n_c, SSM_CHUNK, *a.shape[2:]).swapaxes(0, 1)

    causal = jnp.tril(jnp.ones((SSM_CHUNK, SSM_CHUNK), bool))

    def step(state, inp):
        xc, dtc, bc, cc = inp
        cum = jnp.cumsum(dtc * a_neg, axis=1)
        diff = cum[:, :, None, :] - cum[:, None, :, :]
        decay = jnp.exp(jnp.where(causal[None, :, :, None], diff, -jnp.inf))
        cb = jnp.repeat(jnp.einsum('btgn,bsgn->btsg', cc, bc), SSM_HEADS_PER_GROUP, axis=-1)
        y_intra = jnp.einsum('btsh,bshp->bthp', cb * decay * dtc[:, None, :, :], xc)
        ch = jnp.repeat(cc, SSM_HEADS_PER_GROUP, axis=2)
        bh = jnp.repeat(bc, SSM_HEADS_PER_GROUP, axis=2)
        y_inter = jnp.einsum('bthn,bhpn->bthp', ch, state) * jnp.exp(cum)[..., None]
        w_end = jnp.exp(cum[:, -1:, :] - cum) * dtc
        new_state = state * jnp.exp(cum[:, -1, :])[:, :, None, None] + jnp.einsum('bsh,bshn,bshp->bhpn', w_end, bh, xc)
        return new_state, y_intra + y_inter

    state0 = jnp.zeros((b, SSM_HEADS, SSM_HEAD_DIM, SSM_STATE), jnp.float32)
    _, y = lax.scan(step, state0, (chunks(xs), chunks(dt), chunks(bm), chunks(cm)))
    return y.swapaxes(0, 1).reshape(b, n_c * SSM_CHUNK, SSM_HEADS, SSM_HEAD_DIM)[:, :s]


def mamba2_mixer(z, xbc, dt_raw, conv_w, conv_b, dt_bias, a_log, d_skip, g_ssm):
    b, s = z.shape[:2]
    xbc = lax.conv_general_dilated(xbc, conv_w[:, None, :], window_strides=(1,), padding=[(SSM_CONV - 1, 0)],
                                   dimension_numbers=('NWC', 'WIO', 'NWC'), feature_group_count=SSM_CONV_DIM) + conv_b
    xbc = jax.nn.silu(xbc)
    xs, bm, cm = jnp.split(xbc, [SSM_INNER, SSM_INNER + SSM_GROUPS * SSM_STATE], axis=-1)
    xs = xs.reshape(b, s, SSM_HEADS, SSM_HEAD_DIM).astype(jnp.float32)
    bm = bm.reshape(b, s, SSM_GROUPS, SSM_STATE).astype(jnp.float32)
    cm = cm.reshape(b, s, SSM_GROUPS, SSM_STATE).astype(jnp.float32)
    dt = jax.nn.softplus(dt_raw.astype(jnp.float32) + dt_bias.astype(jnp.float32))
    a_neg = -jnp.exp(a_log.astype(jnp.float32))
    y = ssd_scan(xs, dt, a_neg, bm, cm)
    y = y + d_skip.astype(jnp.float32)[:, None] * xs
    y = y.reshape(b, s, SSM_INNER) * jax.nn.silu(z.astype(jnp.float32))
    yg = y.reshape(b, s, SSM_GROUPS, SSM_INNER // SSM_GROUPS)
    yg = yg * lax.rsqrt(jnp.mean(yg * yg, axis=-1, keepdims=True) + RMS_EPS)
    y = yg.reshape(b, s, SSM_INNER) * g_ssm.astype(jnp.float32)
    return y.astype(z.dtype)


def peer_ffn(h, w_peer_q, peer_keys, peer_u, peer_v):
    b, s, _ = h.shape
    q = (h @ w_peer_q).reshape(b, s, PEER_HEADS, 2, PEER_KEY_DIM)
    sc = jnp.einsum('bshcd,hckd->bshck', q, peer_keys).astype(jnp.float32)
    s1, i1 = lax.top_k(sc[..., 0, :], PEER_TOPK)
    s2, i2 = lax.top_k(sc[..., 1, :], PEER_TOPK)
    cand_s = (s1[..., :, None] + s2[..., None, :]).reshape(b, s, PEER_HEADS, PEER_TOPK * PEER_TOPK)
    cand_i = (i1[..., :, None] * PEER_NKEYS + i2[..., None, :]).reshape(b, s, PEER_HEADS, PEER_TOPK * PEER_TOPK)
    top_s, pos = lax.top_k(cand_s, PEER_TOPK)
    eid = jnp.take_along_axis(cand_i, pos, axis=-1)
    gate = jax.nn.softmax(top_s, axis=-1).astype(h.dtype)

    def block(args):
        hb, eb, gb = args
        u = peer_u[eb]
        act = jax.nn.gelu(jnp.einsum('bthkd,btd->bthk', u, hb), approximate=False)
        vv = peer_v[eb]
        return jnp.einsum('bthk,bthkd->btd', gb * act, vv)

    out = lax.map(block, (to_blocks(h, PEER_BLOCK), to_blocks(eid, PEER_BLOCK), to_blocks(gate, PEER_BLOCK)))
    return from_blocks(out)


def hybrid_layer(x, g_norm1, w_in, g_cq, g_ckv, w_uq, w_ukv, g_qn, g_kn, w_iq, w_o_attn,
                 conv_w, conv_b, dt_bias, a_log, d_skip, g_ssm, w_o_ssm, w_out,
                 g_norm2, w_peer_q, peer_keys, peer_u, peer_v):
    b, s, _ = x.shape
    h = rms_norm(x, g_norm1)
    proj = h @ w_in
    c_q, c_kv, k_idx, w_idx, z, xbc, dt_raw, gate_a, gate_b = jnp.split(proj, IN_OFFSETS, axis=-1)
    cq = rms_norm(c_q, g_cq)
    ckv = rms_norm(c_kv, g_ckv)
    q = rms_norm((cq @ w_uq).reshape(b, s, ATTN_HEADS, ATTN_HEAD_DIM), g_qn)
    kv = (ckv @ w_ukv).reshape(b, s, ATTN_HEADS, 2 * ATTN_HEAD_DIM)
    k = rms_norm(kv[..., :ATTN_HEAD_DIM], g_kn)
    v = kv[..., ATTN_HEAD_DIM:]
    q_idx = (cq @ w_iq).reshape(b, s, IDX_HEADS, IDX_DIM)
    attn = sparse_attention(q, k, v, q_idx, k_idx, w_idx * IDX_HEADS ** -0.5)
    y_a = attn.reshape(b, s, ATTN_HEADS * ATTN_HEAD_DIM) @ w_o_attn
    y_b = mamba2_mixer(z, xbc, dt_raw, conv_w, conv_b, dt_bias, a_log, d_skip, g_ssm) @ w_o_ssm
    mixed = jax.nn.sigmoid(gate_a) * y_a + jax.nn.sigmoid(gate_b) * y_b
    x = x + mixed @ w_out
    x = x + peer_ffn(rms_norm(x, g_norm2), w_peer_q, peer_keys, peer_u, peer_v)
    return x


def setup_inputs(seed: int = 0) -> dict:
    key = jax.random.key(seed)
    ks = jax.random.split(key, 26)
    L = DEPTH
    f32 = jnp.float32

    def nrm(k, shape, fan_in):
        return jax.random.normal(k, shape, f32) * (fan_in ** -0.5)

    def gain(k, n):
        return 1.0 + 0.02 * jax.random.normal(k, (L, n), f32)

    dt0 = jnp.exp(jax.random.uniform(ks[13], (L, SSM_HEADS), f32, minval=float(np.log(1e-3)), maxval=float(np.log(1e-1))))
    return {
        "x": jax.random.normal(ks[0], (BATCH, SEQ, D_MODEL), f32),
        "g_norm1": gain(ks[1], D_MODEL),
        "w_in": nrm(ks[2], (L, D_MODEL, IN_WIDTH), D_MODEL),
        "g_cq": gain(ks[3], Q_LORA),
        "g_ckv": gain(ks[4], KV_LORA),
        "w_uq": nrm(ks[5], (L, Q_LORA, ATTN_HEADS * ATTN_HEAD_DIM), Q_LORA),
        "w_ukv": nrm(ks[6], (L, KV_LORA, ATTN_HEADS * 2 * ATTN_HEAD_DIM), KV_LORA),
        "g_qn": gain(ks[7], ATTN_HEAD_DIM),
        "g_kn": gain(ks[8], ATTN_HEAD_DIM),
        "w_iq": nrm(ks[9], (L, Q_LORA, IDX_HEADS * IDX_DIM), Q_LORA),
        "w_o_attn": nrm(ks[10], (L, ATTN_HEADS * ATTN_HEAD_DIM, D_MODEL), ATTN_HEADS * ATTN_HEAD_DIM),
        "conv_w": nrm(ks[11], (L, SSM_CONV, SSM_CONV_DIM), SSM_CONV),
        "conv_b": 0.02 * jax.random.normal(ks[12], (L, SSM_CONV_DIM), f32),
        "dt_bias": dt0 + jnp.log(-jnp.expm1(-dt0)),
        "a_log": jnp.log(jax.random.uniform(ks[14], (L, SSM_HEADS), f32, minval=1.0, maxval=16.0)),
        "d_skip": 1.0 + 0.1 * jax.random.normal(ks[15], (L, SSM_HEADS), f32),
        "g_ssm": gain(ks[16], SSM_INNER),
        "w_o_ssm": nrm(ks[17], (L, SSM_INNER, D_MODEL), SSM_INNER),
        "w_out": nrm(ks[18], (L, D_MODEL, D_MODEL), D_MODEL),
        "g_norm2": gain(ks[19], D_MODEL),
        "w_peer_q": nrm(ks[20], (L, D_MODEL, PEER_HEADS * 2 * PEER_KEY_DIM), D_MODEL),
        "peer_keys": nrm(ks[21], (L, PEER_HEADS, 2, PEER_NKEYS, PEER_KEY_DIM), PEER_KEY_DIM),
        "peer_u": nrm(ks[22], (L, PEER_EXPERTS, D_MODEL), D_MODEL),
        "peer_v": nrm(ks[23], (L, PEER_EXPERTS, D_MODEL), PEER_HEADS),
    }


def reference(x, g_norm1, w_in, g_cq, g_ckv, w_uq, w_ukv, g_qn, g_kn, w_iq, w_o_attn,
              conv_w, conv_b, dt_bias, a_log, d_skip, g_ssm, w_o_ssm, w_out,
              g_norm2, w_peer_q, peer_keys, peer_u, peer_v):
    for layer in range(DEPTH):
        x = hybrid_layer(x, g_norm1[layer], w_in[layer], g_cq[layer], g_ckv[layer], w_uq[layer], w_ukv[layer],
                         g_qn[layer], g_kn[layer], w_iq[layer], w_o_attn[layer],
                         conv_w[layer], conv_b[layer], dt_bias[layer], a_log[layer], d_skip[layer], g_ssm[layer],
                         w_o_ssm[layer], w_out[layer], g_norm2[layer], w_peer_q[layer], peer_keys[layer],
                         peer_u[layer], peer_v[layer])
    return x
```

```python
import functools

import numpy as np
import jax
import jax.numpy as jnp
from jax import lax
from jax.experimental import pallas as pl
from jax.experimental.pallas import tpu as pltpu

F32 = jnp.float32
BF16 = jnp.bfloat16
I32 = jnp.int32

LANES = 128
VMEM_LIMIT = 56 * 1024 * 1024

RMS_EPS = 1e-6
ATTN_HEADS = 16
ATTN_HEAD_DIM = 128
Q_LORA = 512
KV_LORA = 256
IDX_HEADS = 16
IDX_DIM = 64
INDEX_TOPK = 256
SSM_HEAD_DIM = 64
SSM_GROUPS = 8
SSM_STATE = 128
SSM_CONV = 4
SSM_CHUNK = 256
PEER_HEADS = 8
PEER_NKEYS = 128
PEER_KEY_DIM = 128
PEER_TOPK = 16

NEG_BIG = -1e30
M_INIT = -3e38
INT_MIN = -(2 ** 31)
INT_MAX = 2 ** 31 - 1


def _params(sem):
    return pltpu.CompilerParams(dimension_semantics=sem, vmem_limit_bytes=VMEM_LIMIT)


def _rms(x, g):
    ms = jnp.mean(x * x, axis=-1, keepdims=True)
    return x * lax.rsqrt(ms + RMS_EPS) * g


def _silu(x):
    return x * (1.0 / (1.0 + jnp.exp(-x)))


def _softplus(x):
    return jnp.maximum(x, 0.0) + jnp.log1p(jnp.exp(-jnp.abs(x)))


def _dot(a, b):
    return jnp.dot(a, b, preferred_element_type=F32)


def _dot_nt(a, b):
    return lax.dot_general(a, b, (((1,), (1,)), ((), ())), preferred_element_type=F32)


def _dot_tn(a, b):
    return lax.dot_general(a, b, (((0,), (0,)), ((), ())), preferred_element_type=F32)


def _rms_matmul_kernel(x_ref, g_ref, w_ref, *refs, emit_norm):
    if emit_norm:
        o_ref, n_ref, xn_ref = refs
    else:
        o_ref, xn_ref = refs

    @pl.when(pl.program_id(1) == 0)
    def _():
        y = _rms(x_ref[...], g_ref[...]).astype(BF16)
        xn_ref[...] = y
        if emit_norm:
            n_ref[...] = y

    o_ref[...] = _dot(xn_ref[...], w_ref[...]).astype(o_ref.dtype)


def _rms_matmul(x, g, w, *, tm, tn, out_dtype, emit_norm=False):
    m, k = x.shape
    n = w.shape[1]
    assert m % tm == 0 and n % tn == 0
    out_shape = [jax.ShapeDtypeStruct((m, n), out_dtype)]
    out_specs = [pl.BlockSpec((tm, tn), lambda i, j: (i, j))]
    if emit_norm:
        out_shape.append(jax.ShapeDtypeStruct((m, k), BF16))
        out_specs.append(pl.BlockSpec((tm, k), lambda i, j: (i, 0)))
    res = pl.pallas_call(
        functools.partial(_rms_matmul_kernel, emit_norm=emit_norm),
        grid=(m // tm, n // tn),
        in_specs=[
            pl.BlockSpec((tm, k), lambda i, j: (i, 0)),
            pl.BlockSpec((1, k), lambda i, j: (0, 0)),
            pl.BlockSpec((k, tn), lambda i, j: (0, j)),
        ],
        out_specs=out_specs,
        out_shape=out_shape,
        scratch_shapes=[pltpu.VMEM((tm, k), BF16)],
        compiler_params=_params(("parallel", "arbitrary")),
        name="rms_matmul",
    )(x, g, w)
    return res if emit_norm else res[0]


def _attn_prep_kernel(cq_ref, ckv_ref, gcq_ref, gckv_ref, wuq_ref, wuk_ref, wuv_ref, wiq_ref,
                      gqn_ref, gkn_ref, q_ref, k_ref, v_ref, qi_ref):
    scale = ATTN_HEAD_DIM ** -0.5
    cq = _rms(cq_ref[...], gcq_ref[...]).astype(BF16)
    ckv = _rms(ckv_ref[...], gckv_ref[...]).astype(BF16)
    qf = _dot(cq, wuq_ref[...])
    kf = _dot(ckv, wuk_ref[...])
    gqn = gqn_ref[...]
    gkn = gkn_ref[...]
    for h in range(ATTN_HEADS):
        sl = slice(h * ATTN_HEAD_DIM, (h + 1) * ATTN_HEAD_DIM)
        q_ref[:, sl] = (_rms(qf[:, sl], gqn) * scale).astype(BF16)
        k_ref[:, sl] = _rms(kf[:, sl], gkn).astype(BF16)
    v_ref[...] = _dot(ckv, wuv_ref[...]).astype(BF16)
    qi = _dot(cq, wiq_ref[...]).astype(BF16)
    for h in range(IDX_HEADS):
        qi_ref[h] = qi[:, h * LANES:(h + 1) * LANES]


def _attn_prep(small, g_cq, g_ckv, w_uq, w_uk, w_uv, w_iqp, g_qn, g_kn, *, tm):
    s = small.shape[0]
    d_att = ATTN_HEADS * ATTN_HEAD_DIM
    full = lambda shape: pl.BlockSpec(shape, lambda i: (0,) * len(shape))
    return pl.pallas_call(
        _attn_prep_kernel,
        grid=(s // tm,),
        in_specs=[
            pl.BlockSpec((tm, Q_LORA), lambda i: (i, 0)),
            pl.BlockSpec((tm, KV_LORA), lambda i: (i, Q_LORA // KV_LORA)),
            full((1, Q_LORA)), full((1, KV_LORA)),
            full((Q_LORA, d_att)), full((KV_LORA, d_att)), full((KV_LORA, d_att)),
            full((Q_LORA, IDX_HEADS * LANES)),
            full((1, ATTN_HEAD_DIM)), full((1, ATTN_HEAD_DIM)),
        ],
        out_specs=[
            pl.BlockSpec((tm, d_att), lambda i: (i, 0)),
            pl.BlockSpec((tm, d_att), lambda i: (i, 0)),
            pl.BlockSpec((tm, d_att), lambda i: (i, 0)),
            pl.BlockSpec((IDX_HEADS, tm, LANES), lambda i: (0, i, 0)),
        ],
        out_shape=[
            jax.ShapeDtypeStruct((s, d_att), BF16),
            jax.ShapeDtypeStruct((s, d_att), BF16),
            jax.ShapeDtypeStruct((s, d_att), BF16),
            jax.ShapeDtypeStruct((IDX_HEADS, s, LANES), BF16),
        ],
        compiler_params=_params(("parallel",)),
        name="attn_prep",
    )(small, small, g_cq, g_ckv, w_uq, w_uk, w_uv, w_iqp, g_qn, g_kn)


def _indexer_kernel(qi_ref, kt_ref, w_ref, bias_ref, keys_ref, res_ref, wb_ref,
                    *, tq, tk, n_kt, k_top, rb):
    i = pl.program_id(0)
    nkv = (i * tq + tq - 1) // tk + 1
    nc = tk // LANES

    w = w_ref[...]
    for h in range(IDX_HEADS):
        wb_ref[h] = jnp.broadcast_to(w[:, h:h + 1], (tq, LANES))
    qall = qi_ref[...].reshape(IDX_HEADS * tq, LANES)

    def score_tile(j, carry):
        res_ref[...] = _dot(qall, kt_ref[j])
        for r0 in range(0, tq, rb):
            acc = jnp.zeros((rb, tk), F32)
            for h in range(IDX_HEADS):
                r = res_ref[h * tq + r0:h * tq + r0 + rb, :]
                wbh = wb_ref[h, r0:r0 + rb, :]
                acc = acc + jnp.maximum(r, 0.0) * jnp.concatenate([wbh] * nc, axis=1)
            b = lax.bitcast_convert_type(acc, I32)
            key = jnp.where(b < 0, b ^ INT_MAX, b)
            row = i * tq + r0 + lax.broadcasted_iota(I32, (rb, tk), 0)
            col = j * tk + lax.broadcasted_iota(I32, (rb, tk), 1)
            keys_ref[j, r0:r0 + rb, :] = jnp.where(col <= row, key, INT_MIN)
        return carry

    lax.fori_loop(0, nkv, score_tile, 0)

    def count_ge(mid):
        def body(j, cnt):
            kt = keys_ref[j]
            for c in range(nc):
                cnt = cnt + jnp.where(kt[:, c * LANES:(c + 1) * LANES] >= mid, 1.0, 0.0)
            return cnt
        cnt = lax.fori_loop(0, nkv, body, jnp.zeros((tq, LANES), F32))
        return jnp.broadcast_to(jnp.sum(cnt, axis=-1, keepdims=True), (tq, LANES))

    def cond(c):
        return jnp.logical_and(c[0] < 40, c[3] > 0.0)

    def step(c):
        it, lo, hi, _ = c
        mid = (lo | hi) - ((lo ^ hi) >> 1)
        tot = count_ge(mid)
        ge = tot >= float(k_top)
        eq = tot == float(k_top)
        lo_n = jnp.where(ge, mid, lo)
        hi_n = jnp.where(eq, mid, jnp.where(ge, hi, mid - 1))
        open_rows = jnp.max(jnp.where(lo_n != hi_n, 1.0, 0.0))
        return it + 1, lo_n, hi_n, open_rows

    init = (jnp.int32(0), jnp.full((tq, LANES), INT_MIN + 1, I32), jnp.full((tq, LANES), INT_MAX, I32),
            jnp.float32(1.0))
    thr = lax.while_loop(cond, step, init)[1]

    for jt in range(n_kt):
        @pl.when(jt < nkv)
        def _():
            kt = keys_ref[jt]
            for c in range(nc):
                sel = kt[:, c * LANES:(c + 1) * LANES] >= thr
                bias_ref[:, jt * tk + c * LANES:jt * tk + (c + 1) * LANES] = (
                    jnp.where(sel, 0.0, NEG_BIG).astype(BF16))

        @pl.when(jt >= nkv)
        def _():
            bias_ref[:, jt * tk:(jt + 1) * tk] = jnp.full((tq, tk), NEG_BIG, BF16)


def _indexer(qi, kt, w_idx, *, tq, tk, k_top):
    _, s, _ = qi.shape
    n_kt = s // tk
    return pl.pallas_call(
        functools.partial(_indexer_kernel, tq=tq, tk=tk, n_kt=n_kt, k_top=k_top, rb=min(32, tq)),
        grid=(s // tq,),
        in_specs=[
            pl.BlockSpec((IDX_HEADS, tq, LANES), lambda i: (0, i, 0)),
            pl.BlockSpec((n_kt, LANES, tk), lambda i: (0, 0, 0)),
            pl.BlockSpec((tq, IDX_HEADS), lambda i: (i, 0)),
        ],
        out_specs=pl.BlockSpec((tq, s), lambda i: (i, 0)),
        out_shape=jax.ShapeDtypeStruct((s, s), BF16),
        scratch_shapes=[
            pltpu.VMEM((n_kt, tq, tk), I32),
            pltpu.VMEM((IDX_HEADS * tq, tk), F32),
            pltpu.VMEM((IDX_HEADS, tq, LANES), F32),
        ],
        compiler_params=_params(("arbitrary",)),
        name="indexer",
    )(qi, kt, w_idx)


def _attn_kernel(q_ref, k_ref, v_ref, b_ref, o_ref, acc_ref, m_ref, l_ref, *, tq, tk, n_k):
    i = pl.program_id(0)
    j = pl.program_id(1)

    @pl.when(j == 0)
    def _():
        acc_ref[...] = jnp.zeros_like(acc_ref)
        m_ref[...] = jnp.full_like(m_ref, M_INIT)
        l_ref[...] = jnp.zeros_like(l_ref)

    last = (i * tq + tq - 1) // tk

    @pl.when(j <= last)
    def _():
        bias = b_ref[...].astype(F32)
        for h in range(ATTN_HEADS):
            sl = slice(h * ATTN_HEAD_DIM, (h + 1) * ATTN_HEAD_DIM)
            s = _dot_nt(q_ref[:, sl], k_ref[:, sl]) + bias
            m_prev = m_ref[:, sl]
            m_new = jnp.maximum(m_prev, jnp.max(s, axis=-1, keepdims=True))
            alpha = jnp.exp(m_prev - m_new)
            p = jnp.exp(s - m_new[:, :1])
            l_ref[:, sl] = alpha * l_ref[:, sl] + jnp.sum(p, axis=-1, keepdims=True)
            acc_ref[:, sl] = alpha * acc_ref[:, sl] + _dot(p.astype(BF16), v_ref[:, sl])
            m_ref[:, sl] = m_new

    @pl.when(j == n_k - 1)
    def _():
        o_ref[...] = (acc_ref[...] / l_ref[...]).astype(o_ref.dtype)


def _attention(q, k, v, bias, *, tq, tk):
    s, d = q.shape
    n_k = s // tk

    def kv_map(i, j):
        return (jnp.minimum(j, (i * tq + tq - 1) // tk), 0)

    def b_map(i, j):
        return (i, jnp.minimum(j, (i * tq + tq - 1) // tk))

    return pl.pallas_call(
        functools.partial(_attn_kernel, tq=tq, tk=tk, n_k=n_k),
        grid=(s // tq, n_k),
        in_specs=[
            pl.BlockSpec((tq, d), lambda i, j: (i, 0)),
            pl.BlockSpec((tk, d), kv_map),
            pl.BlockSpec((tk, d), kv_map),
            pl.BlockSpec((tq, tk), b_map),
        ],
        out_specs=pl.BlockSpec((tq, d), lambda i, j: (i, 0)),
        out_shape=jax.ShapeDtypeStruct((s, d), BF16),
        scratch_shapes=[pltpu.VMEM((tq, d), F32), pltpu.VMEM((tq, d), F32), pltpu.VMEM((tq, d), F32)],
        compiler_params=_params(("parallel", "arbitrary")),
        name="sparse_attn",
    )(q, k, v, bias)


def _split3(x):
    hi = x.astype(BF16)
    r1 = x - hi.astype(F32)
    mid = r1.astype(BF16)
    lo = (r1 - mid.astype(F32)).astype(BF16)
    return hi, mid, lo


def _ssd_kernel(xs_ref, b_ref, c_ref, wx_ref, wb_ref, wc_ref, bx_ref, bb_ref, bc_ref,
                dt_ref, dtt_ref, dtb_ref, dtbt_ref, alog_ref, alogt_ref, dsk_ref, z_ref, gs_ref,
                o_ref, cx_ref, cb_ref, cc_ref, st_ref, ex_ref, eb_ref, ec_ref, y_ref, *, L):
    c = pl.program_id(0)
    g = pl.program_id(1)
    hpg = 8
    halo = 8

    @pl.when(c == 0)
    def _():
        cx_ref[g] = jnp.zeros(cx_ref.shape[1:], F32)
        cb_ref[g] = jnp.zeros(cb_ref.shape[1:], F32)
        cc_ref[g] = jnp.zeros(cc_ref.shape[1:], F32)
        st_ref[g] = jnp.zeros(st_ref.shape[1:], F32)

    def conv_silu(x_ref, carry_ref, w_ref, bias_ref, ext_ref):
        ext_ref[0:halo, :] = carry_ref[g]
        ext_ref[halo:halo + L, :] = x_ref[...]
        carry_ref[g] = x_ref[L - halo:L, :]
        w = w_ref[...]
        y = bias_ref[...] + w[SSM_CONV - 1:SSM_CONV, :] * ext_ref[halo:halo + L, :]
        for d in range(1, SSM_CONV):
            y = y + w[SSM_CONV - 1 - d:SSM_CONV - d, :] * ext_ref[halo - d:halo - d + L, :]
        return _silu(y)

    xs = conv_silu(xs_ref, cx_ref, wx_ref, bx_ref, ex_ref)
    bm = conv_silu(b_ref, cb_ref, wb_ref, bb_ref, eb_ref).astype(BF16)
    cm = conv_silu(c_ref, cc_ref, wc_ref, bc_ref, ec_ref).astype(BF16)

    dt = _softplus(dt_ref[0] + dtb_ref[0])
    da = dt * (-jnp.exp(alog_ref[0]))
    dtt = _softplus(dtt_ref[0] + dtbt_ref[0])
    dat = dtt * (-jnp.exp(alogt_ref[0]))
    ri = lax.broadcasted_iota(I32, (L, L), 0)
    ci = lax.broadcasted_iota(I32, (L, L), 1)
    causal = ci <= ri
    tril = jnp.where(causal, 1.0, 0.0).astype(BF16)
    triu = jnp.where(ri <= ci, 1.0, 0.0).astype(BF16)
    cum = sum(_dot(tril, part) for part in _split3(da))
    cumt = sum(_dot(part, triu) for part in _split3(dat))

    cb = _dot_nt(cm, bm)
    lane = lax.broadcasted_iota(I32, (L, LANES), 1)
    low = lane < SSM_HEAD_DIM
    lane_row = lax.broadcasted_iota(I32, (1, LANES), 1) < SSM_HEAD_DIM
    clast = cum[L - 1:L, :]

    for p in range(hpg // 2):
        xs_p = xs[:, p * LANES:(p + 1) * LANES]
        halves = (jnp.where(low, xs_p, 0.0).astype(BF16), jnp.where(low, 0.0, xs_p).astype(BF16))
        y_intra = jnp.zeros((L, LANES), F32)
        for q in range(2):
            jh = 2 * p + q
            diff = cum[:, jh:jh + 1] - cumt[jh:jh + 1, :]
            decay = jnp.where(causal, jnp.exp(diff), 0.0)
            mat = (cb * decay * dtt[jh:jh + 1, :]).astype(BF16)
            y_intra = y_intra + _dot(mat, halves[q])
        c0 = cum[:, 2 * p:2 * p + 1]
        c1 = cum[:, 2 * p + 1:2 * p + 2]
        e_t = jnp.where(low, jnp.exp(c0), jnp.exp(c1))
        state = st_ref[g, p]
        y_inter = _dot(cm, state.astype(BF16)) * e_t
        y_ref[:, p * LANES:(p + 1) * LANES] = y_intra + y_inter + dsk_ref[:, p * LANES:(p + 1) * LANES] * xs_p
        l0 = clast[:, 2 * p:2 * p + 1]
        l1 = clast[:, 2 * p + 1:2 * p + 2]
        w_end = jnp.where(low, jnp.exp(l0 - c0) * dt[:, 2 * p:2 * p + 1],
                          jnp.exp(l1 - c1) * dt[:, 2 * p + 1:2 * p + 2])
        upd = _dot_tn(bm, (xs_p * w_end).astype(BF16))
        st_ref[g, p] = state * jnp.where(lane_row, jnp.exp(l0), jnp.exp(l1)) + upd

    z = z_ref[...]
    y = y_ref[...] * _silu(z)
    y = y * lax.rsqrt(jnp.mean(y * y, axis=-1, keepdims=True) + RMS_EPS)
    o_ref[...] = (y * gs_ref[...]).astype(o_ref.dtype)


def _ssd(xbc, conv_w, conv_b, dt_g, dt_gt, dtb_g, dtb_gt, alog_g, alog_gt, dskip_e, z, g_ssm):
    s = xbc.shape[0]
    L = SSM_CHUNK
    d_inner = z.shape[1]
    gw = d_inner // SSM_GROUPS
    nb_x = d_inner // LANES
    nb_c = nb_x + SSM_GROUPS
    return pl.pallas_call(
        functools.partial(_ssd_kernel, L=L),
        grid=(s // L, SSM_GROUPS),
        in_specs=[
            pl.BlockSpec((L, gw), lambda c, g: (c, g)),
            pl.BlockSpec((L, SSM_STATE), lambda c, g: (c, nb_x + g)),
            pl.BlockSpec((L, SSM_STATE), lambda c, g: (c, nb_c + g)),
            pl.BlockSpec((SSM_CONV, gw), lambda c, g: (0, g)),
            pl.BlockSpec((SSM_CONV, SSM_STATE), lambda c, g: (0, nb_x + g)),
            pl.BlockSpec((SSM_CONV, SSM_STATE), lambda c, g: (0, nb_c + g)),
            pl.BlockSpec((1, gw), lambda c, g: (0, g)),
            pl.BlockSpec((1, SSM_STATE), lambda c, g: (0, nb_x + g)),
            pl.BlockSpec((1, SSM_STATE), lambda c, g: (0, nb_c + g)),
            pl.BlockSpec((1, L, LANES), lambda c, g: (g, c, 0)),
            pl.BlockSpec((1, 8, L), lambda c, g: (g, 0, c)),
            pl.BlockSpec((1, 1, LANES), lambda c, g: (g, 0, 0)),
            pl.BlockSpec((1, 8, 1), lambda c, g: (g, 0, 0)),
            pl.BlockSpec((1, 1, LANES), lambda c, g: (g, 0, 0)),
            pl.BlockSpec((1, 8, 1), lambda c, g: (g, 0, 0)),
            pl.BlockSpec((1, gw), lambda c, g: (0, g)),
            pl.BlockSpec((L, gw), lambda c, g: (c, g)),
            pl.BlockSpec((1, gw), lambda c, g: (0, g)),
        ],
        out_specs=pl.BlockSpec((L, gw), lambda c, g: (c, g)),
        out_shape=jax.ShapeDtypeStruct((s, d_inner), BF16),
        scratch_shapes=[
            pltpu.VMEM((SSM_GROUPS, 8, gw), F32),
            pltpu.VMEM((SSM_GROUPS, 8, SSM_STATE), F32),
            pltpu.VMEM((SSM_GROUPS, 8, SSM_STATE), F32),
            pltpu.VMEM((SSM_GROUPS, 4, SSM_STATE, LANES), F32),
            pltpu.VMEM((L + 8, gw), F32),
            pltpu.VMEM((L + 8, SSM_STATE), F32),
            pltpu.VMEM((L + 8, SSM_STATE), F32),
            pltpu.VMEM((L, gw), F32),
        ],
        compiler_params=_params(("arbitrary", "arbitrary")),
        name="ssd",
    )(xbc, xbc, xbc, conv_w, conv_w, conv_w, conv_b, conv_b, conv_b,
      dt_g, dt_gt, dtb_g, dtb_gt, alog_g, alog_gt, dskip_e, z, g_ssm)


def _matmul_kernel(a_ref, w_ref, *refs, epilogue):
    o_ref = refs[-1]
    extras = [r[...] for r in refs[:-1]]
    o_ref[...] = epilogue(_dot(a_ref[...], w_ref[...]), *extras).astype(o_ref.dtype)


def _matmul(a, w, extras, epilogue, *, tm, tn, out_dtype):
    m, k = a.shape
    n = w.shape[1]
    in_specs = [pl.BlockSpec((tm, k), lambda i, j: (i, 0)), pl.BlockSpec((k, tn), lambda i, j: (0, j))]
    for _, off in extras:
        in_specs.append(pl.BlockSpec((tm, tn), functools.partial(lambda i, j, off: (i, j + off), off=off)))
    return pl.pallas_call(
        functools.partial(_matmul_kernel, epilogue=epilogue),
        grid=(m // tm, n // tn),
        in_specs=in_specs,
        out_specs=pl.BlockSpec((tm, tn), lambda i, j: (i, j)),
        out_shape=jax.ShapeDtypeStruct((m, n), out_dtype),
        compiler_params=_params(("parallel", "arbitrary")),
        name="matmul_epilogue",
    )(a, w, *[e for e, _ in extras])


def _sigmoid(x):
    return 1.0 / (1.0 + jnp.exp(-x))


def _ep_gate(acc, gate):
    return _sigmoid(gate) * acc


def _ep_gate_add(acc, gate, other):
    return _sigmoid(gate) * acc + other


def _ep_residual(acc, res):
    return res + acc


def _top_values(x, k):
    vals = []
    for _ in range(k):
        m = jnp.max(x, axis=0, keepdims=True)
        vals.append(m)
        x = jnp.where(x == m, -jnp.inf, x)
    return vals


def _peer_route_kernel(q_ref, keys_ref, s1_ref, s2_ref, thr_ref, a1_ref, a2_ref, *, T):
    k = PEER_TOPK
    sub = lax.broadcasted_iota(I32, (k, T), 0)
    for h in range(PEER_HEADS):
        tops = []
        for c, out_ref in ((0, s1_ref), (1, s2_ref)):
            hc = 2 * h + c
            sc = _dot_nt(keys_ref[hc], q_ref[:, hc * PEER_KEY_DIM:(hc + 1) * PEER_KEY_DIM])
            vals = _top_values(sc, k)
            out_ref[h] = jnp.where(sc >= vals[k - 1], sc, -jnp.inf)
            tops.append(vals)
        v2 = jnp.zeros((k, T), F32)
        for b in range(k):
            v2 = jnp.where(sub == b, tops[1][b], v2)
        cand = jnp.concatenate([tops[0][a] + v2 for a in range(k)], axis=0)
        best = _top_values(cand, k)
        zsum = jnp.zeros((1, T), F32)
        for b in range(k):
            zsum = zsum + jnp.exp(best[b] - best[0])
        thr_ref[h:h + 1, :] = best[k - 1]
        a1_ref[h:h + 1, :] = tops[0][0] + jnp.log(zsum)
        a2_ref[h:h + 1, :] = tops[1][0]


def _peer_route(q, keys, *, T):
    s = q.shape[0]
    big = jax.ShapeDtypeStruct((PEER_HEADS, PEER_NKEYS, s), F32)
    small = jax.ShapeDtypeStruct((PEER_HEADS, s), F32)
    return pl.pallas_call(
        functools.partial(_peer_route_kernel, T=T),
        grid=(s // T,),
        in_specs=[
            pl.BlockSpec((T, q.shape[1]), lambda i: (i, 0)),
            pl.BlockSpec(keys.shape, lambda i: (0, 0, 0)),
        ],
        out_specs=[
            pl.BlockSpec((PEER_HEADS, PEER_NKEYS, T), lambda i: (0, 0, i)),
            pl.BlockSpec((PEER_HEADS, PEER_NKEYS, T), lambda i: (0, 0, i)),
            pl.BlockSpec((PEER_HEADS, T), lambda i: (0, i)),
            pl.BlockSpec((PEER_HEADS, T), lambda i: (0, i)),
            pl.BlockSpec((PEER_HEADS, T), lambda i: (0, i)),
        ],
        out_shape=[big, big, small, small, small],
        compiler_params=_params(("parallel",)),
        name="peer_route",
    )(q, keys)


def _peer_dense_kernel(h_ref, u_ref, vt_ref, s1_ref, s2_ref, thr_ref, a1_ref, a2_ref, x_ref, o_ref,
                       acc_ref, e1_ref, e2_ref, w_ref, *, T, E, n_e):
    e = pl.program_id(1)

    @pl.when(e == 0)
    def _():
        acc_ref[...] = jnp.zeros_like(acc_ref)
        for h in range(PEER_HEADS):
            e1_ref[h] = jnp.exp(s1_ref[h] - a1_ref[h:h + 1, :])
            e2_ref[h] = jnp.exp(s2_ref[h] - a2_ref[h:h + 1, :])

    a_t = _dot_nt(u_ref[...], h_ref[...])
    act = 0.5 * a_t * (1.0 + lax.erf(a_t * np.float32(np.sqrt(0.5))))
    for ii in range(E // PEER_NKEYS):
        irow = e * (E // PEER_NKEYS) + ii
        gate = jnp.zeros((PEER_NKEYS, T), F32)
        for h in range(PEER_HEADS):
            s1 = s1_ref[h, pl.ds(irow, 1), :]
            e1 = e1_ref[h, pl.ds(irow, 1), :]
            sel = (s1 + s2_ref[h]) >= thr_ref[h:h + 1, :]
            gate = gate + jnp.where(sel, e1 * e2_ref[h], 0.0)
        rows = slice(ii * PEER_NKEYS, (ii + 1) * PEER_NKEYS)
        w_ref[rows, :] = (gate * act[rows, :]).astype(BF16)
    acc_ref[...] += _dot(vt_ref[...], w_ref[...])

    @pl.when(e == n_e - 1)
    def _():
        o_ref[...] = x_ref[...] + acc_ref[...].T


def _peer_dense(h2, u, vt, s1m, s2m, thr, a1, a2, x1, *, T, E):
    s, d = h2.shape
    n_e = u.shape[0] // E
    big = pl.BlockSpec((PEER_HEADS, PEER_NKEYS, T), lambda i, e: (0, 0, i))
    small = pl.BlockSpec((PEER_HEADS, T), lambda i, e: (0, i))
    return pl.pallas_call(
        functools.partial(_peer_dense_kernel, T=T, E=E, n_e=n_e),
        grid=(s // T, n_e),
        in_specs=[
            pl.BlockSpec((T, d), lambda i, e: (i, 0)),
            pl.BlockSpec((E, d), lambda i, e: (e, 0)),
            pl.BlockSpec((d, E), lambda i, e: (0, e)),
            big, big, small, small, small,
            pl.BlockSpec((T, d), lambda i, e: (i, 0)),
        ],
        out_specs=pl.BlockSpec((T, d), lambda i, e: (i, 0)),
        out_shape=jax.ShapeDtypeStruct((s, d), F32),
        scratch_shapes=[
            pltpu.VMEM((d, T), F32),
            pltpu.VMEM((PEER_HEADS, PEER_NKEYS, T), F32),
            pltpu.VMEM((PEER_HEADS, PEER_NKEYS, T), F32),
            pltpu.VMEM((E, T), BF16),
        ],
        compiler_params=_params(("parallel", "arbitrary")),
        name="peer_dense",
    )(h2, u, vt, s1m, s2m, thr, a1, a2, x1)


def _tile(n, pref):
    t = min(pref, n)
    assert n % t == 0
    return t


def _layer(x, g_norm1, w_in, g_cq, g_ckv, w_uq, w_ukv, g_qn, g_kn, w_iq, w_o_attn,
           conv_w, conv_b, dt_bias, a_log, d_skip, g_ssm, w_o_ssm, w_out,
           g_norm2, w_peer_q, peer_keys, peer_u, peer_v):
    s, d_model = x.shape
    d_att = ATTN_HEADS * ATTN_HEAD_DIM
    d_inner = g_ssm.shape[0]
    n_heads = dt_bias.shape[0]
    conv_dim = conv_b.shape[0]
    row = lambda v: v.reshape(1, -1)

    o_ckv = Q_LORA
    o_kidx = o_ckv + KV_LORA
    o_widx = o_kidx + IDX_DIM
    o_z = o_widx + IDX_HEADS
    o_xbc = o_z + d_inner
    o_dt = o_xbc + conv_dim
    o_ga = o_dt + n_heads
    wb = w_in.astype(BF16)
    pad = lambda w, n: jnp.pad(w, ((0, 0), (0, n - w.shape[1])))
    w_small = jnp.concatenate([
        wb[:, :o_kidx], pad(wb[:, o_kidx:o_widx], LANES), pad(wb[:, o_widx:o_z], LANES),
        pad(wb[:, o_dt:o_ga], LANES)], axis=1)
    c_kidx, c_widx, c_dt = o_kidx, o_kidx + LANES, o_kidx + 2 * LANES
    w_z = wb[:, o_z:o_xbc]
    w_xbc = wb[:, o_xbc:o_dt]
    w_gates = wb[:, o_ga:]

    tm = _tile(s, 512)
    small = _rms_matmul(x, row(g_norm1), w_small, tm=tm, tn=w_small.shape[1], out_dtype=F32)
    z = _rms_matmul(x, row(g_norm1), w_z, tm=tm, tn=512, out_dtype=F32)
    xbc = _rms_matmul(x, row(g_norm1), w_xbc, tm=tm, tn=512, out_dtype=F32)
    gates = _rms_matmul(x, row(g_norm1), w_gates, tm=tm, tn=512, out_dtype=F32)

    w_ukv3 = w_ukv.astype(BF16).reshape(KV_LORA, ATTN_HEADS, 2 * ATTN_HEAD_DIM)
    w_uk = w_ukv3[:, :, :ATTN_HEAD_DIM].reshape(KV_LORA, d_att)
    w_uv = w_ukv3[:, :, ATTN_HEAD_DIM:].reshape(KV_LORA, d_att)
    w_iqp = jnp.pad(w_iq.astype(BF16).reshape(Q_LORA, IDX_HEADS, IDX_DIM),
                    ((0, 0), (0, 0), (0, LANES - IDX_DIM))).reshape(Q_LORA, IDX_HEADS * LANES)
    q, k, v, qi = _attn_prep(small, row(g_cq), row(g_ckv), w_uq.astype(BF16), w_uk, w_uv, w_iqp,
                             row(g_qn), row(g_kn), tm=_tile(s, 256))

    tq_i = _tile(s, 128)
    tk_i = _tile(s, 512)
    k_idx = small[:, c_kidx:c_kidx + LANES].astype(BF16)
    kt = k_idx.T.reshape(LANES, s // tk_i, tk_i).transpose(1, 0, 2)
    w_idx = small[:, c_widx:c_widx + IDX_HEADS] * (IDX_HEADS ** -0.5 * IDX_DIM ** -0.5)
    bias = _indexer(qi, kt, w_idx, tq=tq_i, tk=tk_i, k_top=min(INDEX_TOPK, s // 4))
    attn = _attention(q, k, v, bias, tq=_tile(s, 512), tk=_tile(s, 512))

    hpg = n_heads // SSM_GROUPS
    dt_raw = small[:, c_dt:c_dt + n_heads]
    dt_g = jnp.pad(dt_raw.reshape(s, SSM_GROUPS, hpg).transpose(1, 0, 2), ((0, 0), (0, 0), (0, LANES - hpg)))
    dt_gt = dt_raw.T.reshape(SSM_GROUPS, hpg, s)
    grp = lambda p: jnp.pad(p.reshape(SSM_GROUPS, 1, hpg), ((0, 0), (0, 0), (0, LANES - hpg)))
    grpt = lambda p: p.reshape(SSM_GROUPS, hpg, 1)
    dskip_e = jnp.repeat(d_skip, SSM_HEAD_DIM).reshape(1, d_inner)
    yb = _ssd(xbc, conv_w, row(conv_b), dt_g, dt_gt, grp(dt_bias), grpt(dt_bias), grp(a_log), grpt(a_log),
              dskip_e, z, row(g_ssm))

    tm2 = _tile(s, 512)
    n_gb = d_model // 512
    ya = _matmul(attn, w_o_attn.astype(BF16), [(gates, 0)], _ep_gate, tm=tm2, tn=512, out_dtype=F32)
    mixed = _matmul(yb, w_o_ssm.astype(BF16), [(gates, n_gb), (ya, 0)], _ep_gate_add,
                    tm=tm2, tn=512, out_dtype=BF16)
    x1 = _matmul(mixed, w_out.astype(BF16), [(x, 0)], _ep_residual, tm=tm2, tn=512, out_dtype=F32)

    pq, h2 = _rms_matmul(x1, row(g_norm2), w_peer_q.astype(BF16), tm=tm, tn=512, out_dtype=BF16,
                         emit_norm=True)
    keys = peer_keys.astype(BF16).reshape(PEER_HEADS * 2, PEER_NKEYS, PEER_KEY_DIM)
    s1m, s2m, thr, a1, a2 = _peer_route(pq, keys, T=_tile(s, 256))
    return _peer_dense(h2, peer_u.astype(BF16), peer_v.astype(BF16).T, s1m, s2m, thr, a1, a2, x1,
                       T=_tile(s, 512), E=512)


def kernel(x, g_norm1, w_in, g_cq, g_ckv, w_uq, w_ukv, g_qn, g_kn, w_iq, w_o_attn, conv_w, conv_b,
           dt_bias, a_log, d_skip, g_ssm, w_o_ssm, w_out, g_norm2, w_peer_q, peer_keys, peer_u, peer_v):
    b = x.shape[0]
    outs = []
    for bi in range(b):
        xb = x[bi]
        for layer in range(w_in.shape[0]):
            xb = _layer(xb, g_norm1[layer], w_in[layer], g_cq[layer], g_ckv[layer], w_uq[layer],
                        w_ukv[layer], g_qn[layer], g_kn[layer], w_iq[layer], w_o_attn[layer],
                        conv_w[layer], conv_b[layer], dt_bias[layer], a_log[layer], d_skip[layer],
                        g_ssm[layer], w_o_ssm[layer], w_out[layer], g_norm2[layer], w_peer_q[layer],
                        peer_keys[layer], peer_u[layer], peer_v[layer])
        outs.append(xb)
    return jnp.stack(outs, axis=0)
```
